```python
import math
import jax, jax.numpy as jnp
from jax import lax
import numpy as np

D_MODEL = 2048
BATCH = 2
SEQ = 4096
DEPTH = 1

MIX_WIDTH = D_MODEL
ATTN_WIDTH = MIX_WIDTH // 2
CONV_WIDTH = MIX_WIDTH - ATTN_WIDTH
HEAD_DIM = 128
N_HEADS = ATTN_WIDTH // HEAD_DIM
CONV_GROUP = 128
N_CONV_GROUPS = CONV_WIDTH // CONV_GROUP
CONV_K = 3
IN_WIDTH = 3 * ATTN_WIDTH + 3 * CONV_WIDTH
MOBA_BLOCK = 256
MOBA_TOPK = 3
Q_CHUNK = 32
D_FF = 5632
FFN_CONV_K = 3
EPS = 1e-6
NEG_INF = -1e30

kernel_name = "hybrid_moba_shortconv_convffn_block"


def rms_norm(x, g):
    x32 = x.astype(jnp.float32)
    y = x32 * lax.rsqrt(jnp.mean(x32 * x32, axis=-1, keepdims=True) + EPS)
    return (y * g.astype(jnp.float32)).astype(x.dtype)


def modulate(h, shift, scale):
    return h * (1.0 + scale[:, None, :]) + shift[:, None, :]


def causal_dwconv(x, w, b):
    K, C = w.shape
    y = lax.conv_general_dilated(
        x, w[:, None, :].astype(x.dtype), window_strides=(1,), padding=[(K - 1, 0)],
        dimension_numbers=("NWC", "WIO", "NWC"), feature_group_count=C)
    return y + b.astype(x.dtype)


def moba_attention(q, k, v):
    B, H, S, HD = q.shape
    s_pad = ((S + MOBA_BLOCK - 1) // MOBA_BLOCK) * MOBA_BLOCK
    pad = s_pad - S
    padw = ((0, 0), (0, 0), (0, pad), (0, 0))
    q = jnp.pad(q, padw)
    k = jnp.pad(k, padw)
    v = jnp.pad(v, padw)
    nb = s_pad // MOBA_BLOCK
    n_sel = min(MOBA_TOPK, nb)
    scale = 1.0 / math.sqrt(HD)

    kb = k.reshape(B, H, nb, MOBA_BLOCK, HD)
    vb = v.reshape(B, H, nb, MOBA_BLOCK, HD)
    kmean = jnp.mean(kb.astype(jnp.float32), axis=3)
    gate = jnp.einsum("bhsd,bhnd->bhsn", q.astype(jnp.float32), kmean)
    q_blk = jnp.arange(s_pad) // MOBA_BLOCK
    past = jnp.arange(nb)[None, :] < q_blk[:, None]
    gate = jnp.where(past[None, None], gate, NEG_INF)
    _, sel = lax.top_k(gate, n_sel)
    sel_valid = jnp.arange(n_sel)[None, :] < q_blk[:, None]

    base = (jnp.arange(B)[:, None] * H + jnp.arange(H)[None, :]) * nb
    flat_sel = sel + base[:, :, None, None]
    kb_flat = kb.reshape(B * H * nb, MOBA_BLOCK, HD)
    vb_flat = vb.reshape(B * H * nb, MOBA_BLOCK, HD)

    nq = s_pad // Q_CHUNK
    q_ch = q.reshape(B, H, nq, Q_CHUNK, HD).transpose(2, 0, 1, 3, 4)
    sel_ch = flat_sel.reshape(B, H, nq, Q_CHUNK, n_sel).transpose(2, 0, 1, 3, 4)
    valid_ch = sel_valid.reshape(nq, Q_CHUNK, n_sel)
    ids = jnp.arange(nq)

    def step(args):
        qc, selc, validc, ci = args
        blk = (ci * Q_CHUNK) // MOBA_BLOCK
        k_own = lax.dynamic_index_in_dim(kb, blk, axis=2, keepdims=False)
        v_own = lax.dynamic_index_in_dim(vb, blk, axis=2, keepdims=False)
        qpos = ci * Q_CHUNK + jnp.arange(Q_CHUNK)
        kpos = blk * MOBA_BLOCK + jnp.arange(MOBA_BLOCK)
        s_own = jnp.einsum("bhqd,bhkd->bhqk", qc, k_own).astype(jnp.float32) * scale
        s_own = jnp.where((kpos[None, :] <= qpos[:, None])[None, None], s_own, NEG_INF)
        k_sel = kb_flat[selc]
        v_sel = vb_flat[selc]
        s_sel = jnp.einsum("bhqd,bhqnkd->bhqnk", qc, k_sel).astype(jnp.float32) * scale
        s_sel = jnp.where(validc[None, None, :, :, None], s_sel, NEG_INF)
        s_all = jnp.concatenate(
            [s_own, s_sel.reshape(B, H, Q_CHUNK, n_sel * MOBA_BLOCK)], axis=-1)
        p = jax.nn.softmax(s_all, axis=-1)
        p_own = p[..., :MOBA_BLOCK].astype(v.dtype)
        p_sel = p[..., MOBA_BLOCK:].reshape(B, H, Q_CHUNK, n_sel, MOBA_BLOCK).astype(v.dtype)
        return (jnp.einsum("bhqk,bhkd->bhqd", p_own, v_own)
                + jnp.einsum("bhqnk,bhqnkd->bhqd", p_sel, v_sel))

    out = lax.map(step, (q_ch, sel_ch, valid_ch, ids))
    out = out.transpose(1, 2, 0, 3, 4).reshape(B, H, s_pad, HD)
    return out[:, :, :S]


def setup_inputs(seed: int = 0) -> dict:
    key = jax.random.key(seed)
    ks = jax.random.split(key, 20)
    f32 = jnp.float32
    nrm = lambda k, shape, s: jax.random.normal(k, shape, f32) * s
    gain = lambda k, shape: 1.0 + 0.02 * jax.random.normal(k, shape, f32)
    return {
        "x": jax.random.normal(ks[0], (BATCH, SEQ, D_MODEL), f32),
        "c": jax.random.normal(ks[1], (BATCH, D_MODEL), f32),
        "w_ada": nrm(ks[2], (DEPTH, D_MODEL, 6 * D_MODEL), 0.5 * D_MODEL ** -0.5),
        "b_ada": nrm(ks[3], (DEPTH, 6 * D_MODEL), 0.02),
        "norm1_g": gain(ks[4], (DEPTH, D_MODEL)),
        "w_in": nrm(ks[5], (DEPTH, D_MODEL, IN_WIDTH), D_MODEL ** -0.5),
        "q_norm_g": gain(ks[6], (DEPTH, HEAD_DIM)),
        "k_norm_g": gain(ks[7], (DEPTH, HEAD_DIM)),
        "conv_w": nrm(ks[8], (DEPTH, CONV_K, CONV_WIDTH), CONV_K ** -0.5),
        "conv_b": nrm(ks[9], (DEPTH, CONV_WIDTH), 0.02),
        "attn_out_g": gain(ks[10], (DEPTH, ATTN_WIDTH)),
        "conv_out_g": gain(ks[11], (DEPTH, CONV_WIDTH)),
        "w_out": nrm(ks[12], (DEPTH, MIX_WIDTH, D_MODEL), MIX_WIDTH ** -0.5),
        "norm2_g": gain(ks[13], (DEPTH, D_MODEL)),
        "w_ffn_up": nrm(ks[14], (DEPTH, D_MODEL, 2 * D_FF), D_MODEL ** -0.5),
        "ffn_conv_w": nrm(ks[15], (DEPTH, FFN_CONV_K, 2 * D_FF), FFN_CONV_K ** -0.5),
        "ffn_conv_b": nrm(ks[16], (DEPTH, 2 * D_FF), 0.02),
        "w_ffn_down": nrm(ks[17], (DEPTH, D_FF, D_MODEL), D_FF ** -0.5),
    }


def reference(x, c, w_ada, b_ada, norm1_g, w_in, q_norm_g, k_norm_g, conv_w, conv_b,
              attn_out_g, conv_out_g, w_out, norm2_g, w_ffn_up, ffn_conv_w, ffn_conv_b,
              w_ffn_down):
    B, S, D = x.shape
    for l in range(DEPTH):
        mod = jax.nn.silu(c) @ w_ada[l] + b_ada[l]
        sh1, sc1, g1, sh2, sc2, g2 = jnp.split(mod, 6, axis=-1)

        h = modulate(rms_norm(x, norm1_g[l]), sh1, sc1)
        proj = h @ w_in[l]
        q, k, v, gb, gc, xt = jnp.split(
            proj, [ATTN_WIDTH, 2 * ATTN_WIDTH, 3 * ATTN_WIDTH,
                   3 * ATTN_WIDTH + CONV_WIDTH, 3 * ATTN_WIDTH + 2 * CONV_WIDTH], axis=-1)

        q = rms_norm(q.reshape(B, S, N_HEADS, HEAD_DIM), q_norm_g[l]).transpose(0, 2, 1, 3)
        k = rms_norm(k.reshape(B, S, N_HEADS, HEAD_DIM), k_norm_g[l]).transpose(0, 2, 1, 3)
        v = v.reshape(B, S, N_HEADS, HEAD_DIM).transpose(0, 2, 1, 3)
        attn = moba_attention(q, k, v).transpose(0, 2, 1, 3).reshape(B, S, ATTN_WIDTH)

        conv = gb * causal_dwconv(gc * xt, conv_w[l], conv_b[l])

        mixed = jnp.concatenate(
            [rms_norm(attn, attn_out_g[l]), rms_norm(conv, conv_out_g[l])], axis=-1)
        x = x + g1[:, None, :] * (mixed @ w_out[l])

        h2 = modulate(rms_norm(x, norm2_g[l]), sh2, sc2)
        u = causal_dwconv(h2 @ w_ffn_up[l], ffn_conv_w[l], ffn_conv_b[l])
        u_gate, u_up = jnp.split(u, 2, axis=-1)
        ffn = (jax.nn.silu(u_gate) * u_up) @ w_ffn_down[l]
        x = x + g2[:, None, :] * ffn
    return x
```

```python
import functools
import math

import jax
import jax.numpy as jnp
from jax import lax
from jax.experimental import pallas as pl
from jax.experimental.pallas import tpu as pltpu

HEAD_DIM = 128
MOBA_BLOCK = 256
MOBA_TOPK = 3
CONV_K = 3
EPS = 1e-6
NEG_INF = -1e30

V7X_VMEM_BYTES = 64 * 1024 * 1024
VMEM_LIMIT_BYTES = 56 * 1024 * 1024
SUBLANES = 8


def _params(semantics):
    return pltpu.CompilerParams(dimension_semantics=semantics, vmem_limit_bytes=VMEM_LIMIT_BYTES)


def _rms(x, g):
    ms = jnp.mean(x * x, axis=-1, keepdims=True)
    return x * lax.rsqrt(ms + EPS) * g


def _adaln_kernel(c_ref, w_ref, b_ref, o_ref):
    c = c_ref[...]
    a = (c * (1.0 / (1.0 + jnp.exp(-c)))).astype(jnp.bfloat16)
    o_ref[...] = jnp.dot(a, w_ref[...].astype(jnp.bfloat16),
                         preferred_element_type=jnp.float32) + b_ref[...]


def _adaln(c, w_ada, b_ada, tn=1024):
    B, D = c.shape
    N = w_ada.shape[1]
    return pl.pallas_call(
        _adaln_kernel,
        grid=(N // tn,),
        in_specs=[pl.BlockSpec((B, D), lambda j: (0, 0)),
                  pl.BlockSpec((D, tn), lambda j: (0, j)),
                  pl.BlockSpec((1, tn), lambda j: (0, j))],
        out_specs=pl.BlockSpec((B, tn), lambda j: (0, j)),
        out_shape=jax.ShapeDtypeStruct((B, N), jnp.float32),
        compiler_params=_params(("arbitrary",)),
        name="adaln",
    )(c, w_ada, b_ada.reshape(1, N))


def _inproj_kernel(x_ref, mod_ref, n1_ref, w_ref, qg_ref, kg_ref, cw_ref, cb_ref, cog_ref,
                   q_ref, k_ref, vt_ref, km_ref, cv_ref,
                   h_scr, gb_scr, gc_scr, tail_scr, *, tm, scale):
    i = pl.program_id(1)
    j = pl.program_id(2)
    n_heads = q_ref.shape[-1] // HEAD_DIM

    @pl.when(j == 0)
    def _():
        y = _rms(x_ref[...], n1_ref[...])
        h = y * (1.0 + mod_ref[1:2, :]) + mod_ref[0:1, :]
        h_scr[...] = h.astype(jnp.bfloat16)

    r = jnp.dot(h_scr[...], w_ref[...], preferred_element_type=jnp.float32)

    @pl.when(j == 0)
    def _():
        for h in range(n_heads):
            sl = slice(h * HEAD_DIM, (h + 1) * HEAD_DIM)
            q_ref[:, sl] = (_rms(r[:, sl], qg_ref[...]) * scale).astype(q_ref.dtype)

    @pl.when(j == 1)
    def _():
        for h in range(n_heads):
            sl = slice(h * HEAD_DIM, (h + 1) * HEAD_DIM)
            kn = _rms(r[:, sl], kg_ref[...])
            k_ref[:, sl] = kn.astype(k_ref.dtype)
            for blk in range(tm // MOBA_BLOCK):
                rows = kn[blk * MOBA_BLOCK:(blk + 1) * MOBA_BLOCK, :]
                km_ref[blk, :, sl] = jnp.mean(rows, axis=0, keepdims=True)

    @pl.when(j == 2)
    def _():
        vt_ref[...] = r.T.astype(vt_ref.dtype)

    @pl.when(j == 3)
    def _():
        gb_scr[...] = r

    @pl.when(j == 4)
    def _():
        gc_scr[...] = r

    @pl.when((j == 5) & (i == 0))
    def _():
        tail_scr[...] = jnp.zeros_like(tail_scr)

    @pl.when(j == 5)
    def _():
        y = _causal_conv3(gc_scr[...] * r, tail_scr, cw_ref, cb_ref)
        cv_ref[...] = _rms(gb_scr[...] * y, cog_ref[...]).astype(cv_ref.dtype)


def _inproj(x, mod, norm1_g, w_in_bf, q_norm_g, k_norm_g, conv_w, conv_b, conv_out_g, tm=512):
    B, S, D = x.shape
    N = w_in_bf.shape[1]
    gw = N // 6
    nb = S // MOBA_BLOCK
    row = lambda a: a.reshape(1, -1)
    xmap = lambda b, i, j: (b, i, 0)
    const = lambda b, i, j: (0, 0)
    bf = jnp.bfloat16
    return pl.pallas_call(
        functools.partial(_inproj_kernel, tm=tm, scale=1.0 / math.sqrt(HEAD_DIM)),
        grid=(B, S // tm, 6),
        in_specs=[pl.BlockSpec((None, tm, D), xmap),
                  pl.BlockSpec((None, 6, D), lambda b, i, j: (b, 0, 0)),
                  pl.BlockSpec((1, D), const),
                  pl.BlockSpec((D, gw), lambda b, i, j: (0, j)),
                  pl.BlockSpec((1, HEAD_DIM), const),
                  pl.BlockSpec((1, HEAD_DIM), const),
                  pl.BlockSpec((CONV_K, gw), const),
                  pl.BlockSpec((1, gw), const),
                  pl.BlockSpec((1, gw), const)],
        out_specs=[pl.BlockSpec((None, tm, gw), xmap),
                   pl.BlockSpec((None, tm, gw), xmap),
                   pl.BlockSpec((None, gw, tm), lambda b, i, j: (b, 0, i)),
                   pl.BlockSpec((None, tm // MOBA_BLOCK, 1, gw), lambda b, i, j: (b, i, 0, 0)),
                   pl.BlockSpec((None, tm, gw), xmap)],
        out_shape=[jax.ShapeDtypeStruct((B, S, gw), bf),
                   jax.ShapeDtypeStruct((B, S, gw), bf),
                   jax.ShapeDtypeStruct((B, gw, S), bf),
                   jax.ShapeDtypeStruct((B, nb, 1, gw), jnp.float32),
                   jax.ShapeDtypeStruct((B, S, gw), bf)],
        scratch_shapes=[pltpu.VMEM((tm, D), bf),
                        pltpu.VMEM((tm, gw), jnp.float32),
                        pltpu.VMEM((tm, gw), jnp.float32),
                        pltpu.VMEM((SUBLANES, gw), jnp.float32)],
        compiler_params=_params(("arbitrary", "arbitrary", "arbitrary")),
        name="inproj",
    )(x, mod, row(norm1_g), w_in_bf, row(q_norm_g), row(k_norm_g), conv_w, row(conv_b), row(conv_out_g))


def _attn_kernel(q_ref, k_ref, vt_ref, km_ref, o_ref, bias_scr):
    L = MOBA_BLOCK
    nb = q_ref.shape[0] // L
    nt = (((1,), (1,)), ((), ()))
    key_idx = lax.broadcasted_iota(jnp.int32, (L, L), 0)
    qry_idx = lax.broadcasted_iota(jnp.int32, (L, L), 1)
    blk_id = lax.broadcasted_iota(jnp.int32, (nb, L), 0)

    def q_tile(i, carry):
        q0 = pl.multiple_of(i * L, L)
        qi = q_ref[pl.ds(q0, L), :]
        gate = lax.dot_general(km_ref[...], qi.astype(jnp.float32), nt,
                               preferred_element_type=jnp.float32)
        past = blk_id < i
        gate = jnp.where(past, gate, NEG_INF)
        rank = jnp.zeros((nb, L), jnp.int32)
        for n in range(nb):
            gn = gate[n:n + 1, :]
            beats = (gn > gate) | ((gn == gate) & (n < blk_id))
            rank = rank + beats.astype(jnp.int32)
        bias_scr[...] = jnp.where(past & (rank < MOBA_TOPK), 0.0, NEG_INF)

        s = lax.dot_general(k_ref[pl.ds(q0, L), :], qi, nt, preferred_element_type=jnp.float32)
        s = jnp.where(key_idx <= qry_idx, s, NEG_INF)
        m = jnp.max(s, axis=0, keepdims=True)
        p = jnp.exp(s - m)
        l = jnp.sum(p, axis=0, keepdims=True)
        acc = jnp.dot(vt_ref[:, pl.ds(q0, L)], p.astype(jnp.bfloat16),
                      preferred_element_type=jnp.float32)

        def kv_block(n, c):
            m, l, acc = c
            k0 = pl.multiple_of(n * L, L)
            s = lax.dot_general(k_ref[pl.ds(k0, L), :], qi, nt, preferred_element_type=jnp.float32)
            b = bias_scr[pl.ds(n, 1), :]
            m_new = jnp.maximum(m, jnp.max(s, axis=0, keepdims=True) + b)
            alpha = jnp.exp(m - m_new)
            p = jnp.exp(s - (m_new - b))
            l = alpha * l + jnp.sum(p, axis=0, keepdims=True)
            acc = alpha * acc + jnp.dot(vt_ref[:, pl.ds(k0, L)], p.astype(jnp.bfloat16),
                                        preferred_element_type=jnp.float32)
            return m_new, l, acc

        m, l, acc = lax.fori_loop(0, i, kv_block, (m, l, acc))
        o_ref[pl.ds(q0, L), :] = (acc / l).T.astype(o_ref.dtype)
        return carry

    lax.fori_loop(0, nb, q_tile, 0)


def _attention(q, k, vt, kmean):
    B, S, W = q.shape
    H = W // HEAD_DIM
    nb = S // MOBA_BLOCK
    rows = pl.BlockSpec((None, S, HEAD_DIM), lambda b, h: (b, 0, h))
    return pl.pallas_call(
        _attn_kernel,
        grid=(B, H),
        in_specs=[rows, rows,
                  pl.BlockSpec((None, HEAD_DIM, S), lambda b, h: (b, h, 0)),
                  pl.BlockSpec((None, nb, HEAD_DIM), lambda b, h: (b, 0, h))],
        out_specs=rows,
        out_shape=jax.ShapeDtypeStruct((B, S, W), jnp.float32),
        scratch_shapes=[pltpu.VMEM((nb, MOBA_BLOCK), jnp.float32)],
        compiler_params=_params(("arbitrary", "arbitrary")),
        name="moba_attention",
    )(q, k, vt, kmean)


def _outproj_kernel(a_ref, cv_ref, x_ref, mod_ref, ag_ref, w_ref, n2_ref, x1_ref, h2_ref):
    aw = a_ref.shape[-1]
    an = _rms(a_ref[...], ag_ref[...]).astype(jnp.bfloat16)
    y = (jnp.dot(an, w_ref[:aw, :], preferred_element_type=jnp.float32)
         + jnp.dot(cv_ref[...], w_ref[aw:, :], preferred_element_type=jnp.float32))
    x1 = x_ref[...] + mod_ref[2:3, :] * y
    x1_ref[...] = x1
    h2 = _rms(x1, n2_ref[...]) * (1.0 + mod_ref[4:5, :]) + mod_ref[3:4, :]
    h2_ref[...] = h2.astype(h2_ref.dtype)


def _outproj(attn, convn, x, mod, attn_out_g, w_out_bf, norm2_g, tm=512):
    B, S, D = x.shape
    aw = attn.shape[-1]
    row = lambda a: a.reshape(1, -1)
    rmap = lambda b, i: (b, i, 0)
    const = lambda b, i: (0, 0)
    return pl.pallas_call(
        _outproj_kernel,
        grid=(B, S // tm),
        in_specs=[pl.BlockSpec((None, tm, aw), rmap),
                  pl.BlockSpec((None, tm, convn.shape[-1]), rmap),
                  pl.BlockSpec((None, tm, D), rmap),
                  pl.BlockSpec((None, 6, D), lambda b, i: (b, 0, 0)),
                  pl.BlockSpec((1, aw), const),
                  pl.BlockSpec(w_out_bf.shape, const),
                  pl.BlockSpec((1, D), const)],
        out_specs=[pl.BlockSpec((None, tm, D), rmap),
                   pl.BlockSpec((None, tm, D), rmap)],
        out_shape=[jax.ShapeDtypeStruct((B, S, D), jnp.float32),
                   jax.ShapeDtypeStruct((B, S, D), jnp.bfloat16)],
        compiler_params=_params(("arbitrary", "arbitrary")),
        name="outproj",
    )(attn, convn, x, mod, row(attn_out_g), w_out_bf, row(norm2_g))


def _causal_conv3(u, tail_ref, cw_ref, cb_ref):
    tm = u.shape[0]
    row = lax.broadcasted_iota(jnp.int32, u.shape, 0)
    t1 = tail_ref[SUBLANES - 1:SUBLANES, :]
    t2 = tail_ref[SUBLANES - 2:SUBLANES - 1, :]
    u1 = jnp.where(row == 0, t1, pltpu.roll(u, 1, 0))
    u2 = jnp.where(row == 0, t2, jnp.where(row == 1, t1, pltpu.roll(u, 2, 0)))
    tail_ref[...] = u[tm - SUBLANES:, :]
    return cw_ref[0:1, :] * u2 + cw_ref[1:2, :] * u1 + cw_ref[2:3, :] * u + cb_ref[...]


def _ffn_kernel(h2_ref, x1_ref, mod_ref, wg_ref, wu_ref, cwg_ref, cwu_ref, cbg_ref, cbu_ref, wd_ref,
                o_ref, acc_scr, tailg_scr, tailu_scr):
    i = pl.program_id(1)
    j = pl.program_id(2)
    h2 = h2_ref[...]
    ug = jnp.dot(h2, wg_ref[...], preferred_element_type=jnp.float32)
    uu = jnp.dot(h2, wu_ref[...], preferred_element_type=jnp.float32)

    @pl.when(i == 0)
    def _():
        tailg_scr[j] = jnp.zeros(tailg_scr.shape[1:], jnp.float32)
        tailu_scr[j] = jnp.zeros(tailu_scr.shape[1:], jnp.float32)

    cg = _causal_conv3(ug, tailg_scr.at[j], cwg_ref, cbg_ref)
    cu = _causal_conv3(uu, tailu_scr.at[j], cwu_ref, cbu_ref)
    act = (cg * (1.0 / (1.0 + jnp.exp(-cg))) * cu).astype(jnp.bfloat16)
    part = jnp.dot(act, wd_ref[...], preferred_element_type=jnp.float32)

    @pl.when(j == 0)
    def _():
        acc_scr[...] = part

    @pl.when(j > 0)
    def _():
        acc_scr[...] += part

    @pl.when(j == pl.num_programs(2) - 1)
    def _():
        o_ref[...] = x1_ref[...] + mod_ref[5:6, :] * acc_scr[...]


def _ffn(h2, x1, mod, w_up_bf, ffn_conv_w, ffn_conv_b, w_down_bf, tm=512, tf=512):
    B, S, D = x1.shape
    F = w_down_bf.shape[0]
    nf = F // tf
    cb = ffn_conv_b.reshape(1, -1)
    rmap = lambda b, i, j: (b, i, 0)
    gate_cols = lambda b, i, j: (0, j)
    up_cols = lambda b, i, j: (0, j + nf)
    return pl.pallas_call(
        _ffn_kernel,
        grid=(B, S // tm, nf),
        in_specs=[pl.BlockSpec((None, tm, D), rmap),
                  pl.BlockSpec((None, tm, D), rmap),
                  pl.BlockSpec((None, 6, D), lambda b, i, j: (b, 0, 0)),
                  pl.BlockSpec((D, tf), gate_cols),
                  pl.BlockSpec((D, tf), up_cols),
                  pl.BlockSpec((CONV_K, tf), gate_cols),
                  pl.BlockSpec((CONV_K, tf), up_cols),
                  pl.BlockSpec((1, tf), gate_cols),
                  pl.BlockSpec((1, tf), up_cols),
                  pl.BlockSpec((tf, D), lambda b, i, j: (j, 0))],
        out_specs=pl.BlockSpec((None, tm, D), rmap),
        out_shape=jax.ShapeDtypeStruct((B, S, D), jnp.float32),
        scratch_shapes=[pltpu.VMEM((tm, D), jnp.float32),
                        pltpu.VMEM((nf, SUBLANES, tf), jnp.float32),
                        pltpu.VMEM((nf, SUBLANES, tf), jnp.float32)],
        compiler_params=_params(("arbitrary", "arbitrary", "arbitrary")),
        name="convffn",
    )(h2, x1, mod, w_up_bf, w_up_bf, ffn_conv_w, ffn_conv_w, cb, cb, w_down_bf)


def kernel(x, c, w_ada, b_ada, norm1_g, w_in, q_norm_g, k_norm_g, conv_w, conv_b, attn_out_g, conv_out_g,
           w_out, norm2_g, w_ffn_up, ffn_conv_w, ffn_conv_b, w_ffn_down):
    B, S, D = x.shape
    depth = w_ada.shape[0]
    bf = jnp.bfloat16
    for l in range(depth):
        mod = _adaln(c, w_ada[l], b_ada[l]).reshape(B, 6, D)
        q, k, vt, kmean, convn = _inproj(x, mod, norm1_g[l], w_in[l].astype(bf), q_norm_g[l], k_norm_g[l],
                                         conv_w[l], conv_b[l], conv_out_g[l])
        attn = _attention(q, k, vt, kmean.reshape(B, S // MOBA_BLOCK, -1))
        x1, h2 = _outproj(attn, convn, x, mod, attn_out_g[l], w_out[l].astype(bf), norm2_g[l])
        x = _ffn(h2, x1, mod, w_ffn_up[l].astype(bf), ffn_conv_w[l], ffn_conv_b[l], w_ffn_down[l].astype(bf))
    return x
```

```python
import functools
import math

import jax
import jax.numpy as jnp
from jax import lax
from jax.experimental import pallas as pl
from jax.experimental.pallas import tpu as pltpu

HEAD_DIM = 128
MOBA_BLOCK = 256
MOBA_TOPK = 3
CONV_K = 3
EPS = 1e-6
NEG_INF = -1e30
LOG2E = 1.4426950408889634

V7X_VMEM_BYTES = 64 * 1024 * 1024
VMEM_LIMIT_BYTES = 56 * 1024 * 1024
SUBLANES = 8


def _params(semantics):
    return pltpu.CompilerParams(dimension_semantics=semantics, vmem_limit_bytes=VMEM_LIMIT_BYTES)


def _rms(x, g):
    ms = jnp.mean(x * x, axis=-1, keepdims=True)
    return x * lax.rsqrt(ms + EPS) * g


def _adaln_kernel(c_ref, w_ref, b_ref, o_ref):
    c = c_ref[...]
    a = (c * (1.0 / (1.0 + jnp.exp(-c)))).astype(jnp.bfloat16)
    o_ref[...] = jnp.dot(a, w_ref[...].astype(jnp.bfloat16),
                         preferred_element_type=jnp.float32) + b_ref[...]


def _adaln(c, w_ada, b_ada, tn=1024):
    B, D = c.shape
    N = w_ada.shape[1]
    return pl.pallas_call(
        _adaln_kernel,
        grid=(N // tn,),
        in_specs=[pl.BlockSpec((B, D), lambda j: (0, 0)),
                  pl.BlockSpec((D, tn), lambda j: (0, j)),
                  pl.BlockSpec((1, tn), lambda j: (0, j))],
        out_specs=pl.BlockSpec((B, tn), lambda j: (0, j)),
        out_shape=jax.ShapeDtypeStruct((B, N), jnp.float32),
        compiler_params=_params(("arbitrary",)),
        name="adaln",
    )(c, w_ada, b_ada.reshape(1, N))


def _inproj_kernel(x_ref, mod_ref, n1_ref, w_ref, qg_ref, kg_ref, cw_ref, cb_ref, cog_ref,
                   q_ref, k_ref, vt_ref, km_ref, cv_ref,
                   h_scr, gb_scr, gc_scr, tail_scr, *, tm, scale):
    i = pl.program_id(1)
    j = pl.program_id(2)
    n_heads = q_ref.shape[-1] // HEAD_DIM

    @pl.when(j == 0)
    def _():
        y = _rms(x_ref[...], n1_ref[...])
        h = y * (1.0 + mod_ref[1:2, :]) + mod_ref[0:1, :]
        h_scr[...] = h.astype(jnp.bfloat16)

    r = jnp.dot(h_scr[...], w_ref[...], preferred_element_type=jnp.float32)

    @pl.when(j == 0)
    def _():
        for h in range(n_heads):
            sl = slice(h * HEAD_DIM, (h + 1) * HEAD_DIM)
            q_ref[:, sl] = (_rms(r[:, sl], qg_ref[...]) * scale).astype(q_ref.dtype)

    @pl.when(j == 1)
    def _():
        for h in range(n_heads):
            sl = slice(h * HEAD_DIM, (h + 1) * HEAD_DIM)
            kn = _rms(r[:, sl], kg_ref[...])
            k_ref[:, sl] = kn.astype(k_ref.dtype)
            for blk in range(tm // MOBA_BLOCK):
                rows = kn[blk * MOBA_BLOCK:(blk + 1) * MOBA_BLOCK, :]
                km_ref[blk, :, sl] = jnp.mean(rows, axis=0, keepdims=True)

    @pl.when(j == 2)
    def _():
        vt_ref[...] = r.T.astype(vt_ref.dtype)

    @pl.when(j == 3)
    def _():
        gb_scr[...] = r

    @pl.when(j == 4)
    def _():
        gc_scr[...] = r

    @pl.when((j == 5) & (i == 0))
    def _():
        tail_scr[...] = jnp.zeros_like(tail_scr)

    @pl.when(j == 5)
    def _():
        y = _causal_conv3(gc_scr[...] * r, tail_scr, cw_ref, cb_ref)
        cv_ref[...] = _rms(gb_scr[...] * y, cog_ref[...]).astype(cv_ref.dtype)


def _inproj(x, mod, norm1_g, w_in_bf, q_norm_g, k_norm_g, conv_w, conv_b, conv_out_g, tm=512):
    B, S, D = x.shape
    N = w_in_bf.shape[1]
    gw = N // 6
    nb = S // MOBA_BLOCK
    row = lambda a: a.reshape(1, -1)
    xmap = lambda b, i, j: (b, i, 0)
    const = lambda b, i, j: (0, 0)
    bf = jnp.bfloat16
    return pl.pallas_call(
        functools.partial(_inproj_kernel, tm=tm, scale=LOG2E / math.sqrt(HEAD_DIM)),
        grid=(B, S // tm, 6),
        in_specs=[pl.BlockSpec((None, tm, D), xmap),
                  pl.BlockSpec((None, 6, D), lambda b, i, j: (b, 0, 0)),
                  pl.BlockSpec((1, D), const),
                  pl.BlockSpec((D, gw), lambda b, i, j: (0, j)),
                  pl.BlockSpec((1, HEAD_DIM), const),
                  pl.BlockSpec((1, HEAD_DIM), const),
                  pl.BlockSpec((CONV_K, gw), const),
                  pl.BlockSpec((1, gw), const),
                  pl.BlockSpec((1, gw), const)],
        out_specs=[pl.BlockSpec((None, tm, gw), xmap),
                   pl.BlockSpec((None, tm, gw), xmap),
                   pl.BlockSpec((None, gw, tm), lambda b, i, j: (b, 0, i)),
                   pl.BlockSpec((None, tm // MOBA_BLOCK, 1, gw), lambda b, i, j: (b, i, 0, 0)),
                   pl.BlockSpec((None, tm, gw), xmap)],
        out_shape=[jax.ShapeDtypeStruct((B, S, gw), bf),
                   jax.ShapeDtypeStruct((B, S, gw), bf),
                   jax.ShapeDtypeStruct((B, gw, S), bf),
                   jax.ShapeDtypeStruct((B, nb, 1, gw), jnp.float32),
                   jax.ShapeDtypeStruct((B, S, gw), bf)],
        scratch_shapes=[pltpu.VMEM((tm, D), bf),
                        pltpu.VMEM((tm, gw), jnp.float32),
                        pltpu.VMEM((tm, gw), jnp.float32),
                        pltpu.VMEM((SUBLANES, gw), jnp.float32)],
        compiler_params=_params(("arbitrary", "arbitrary", "arbitrary")),
        name="inproj",
    )(x, mod, row(norm1_g), w_in_bf, row(q_norm_g), row(k_norm_g), conv_w, row(conv_b), row(conv_out_g))


ATTN_CHUNK_BLOCKS = 4
ATTN_HEADS_PER_STEP = 2
ONES_ROWS = 16


def _attn_kernel(q_ref, k_ref, vt_ref, km_ref, o_ref, vta_scr, bias_scr, ctile_scr, s_scr, *, hb):
    L = MOBA_BLOCK
    CB = ATTN_CHUNK_BLOCKS
    CK = CB * L
    hd = HEAD_DIM
    S = q_ref.shape[0]
    nb = S // L
    n_items = sum(i // CB + 1 for i in range(nb))
    nt = (((1,), (1,)), ((), ()))
    bf = jnp.bfloat16
    heads = [slice(h * hd, (h + 1) * hd) for h in range(hb)]

    for h in range(hb):
        vta_scr[h, :hd, :] = vt_ref[heads[h], :]
        vta_scr[h, hd:, :] = jnp.ones((ONES_ROWS, S), bf)

    rel = lax.broadcasted_iota(jnp.int32, (CK, L), 0) - lax.broadcasted_iota(jnp.int32, (CK, L), 1)
    for d in range(CB + 1):
        ctile_scr[d] = jnp.where(rel <= d * L, 0.0, NEG_INF) if d < CB else jnp.zeros((CK, L), jnp.float32)

    blk_id = lax.broadcasted_iota(jnp.int32, (nb, L), 0)
    for h in range(hb):
        gate_all = lax.dot_general(km_ref[:, heads[h]], q_ref[:, heads[h]].astype(jnp.float32), nt,
                                   preferred_element_type=jnp.float32)
        for i in range(nb):
            past = blk_id < i
            gate = jnp.where(past, gate_all[:, i * L:(i + 1) * L], NEG_INF)
            rank = jnp.zeros((nb, L), jnp.int32)
            for n in range(i):
                gn = gate[n:n + 1, :]
                beats = (gn > gate) | ((gn == gate) & (n < blk_id))
                rank = rank + beats.astype(jnp.int32)
            bias = jnp.where((past & (rank < MOBA_TOPK)) | (blk_id == i), 0.0, NEG_INF)
            bias_scr[h, i] = bias

    def scores(i, c, slot):
        q0 = pl.multiple_of(i * L, L)
        k0 = pl.multiple_of(c * CK, CK)
        for h in range(hb):
            s_scr[slot, h] = lax.dot_general(k_ref[pl.ds(k0, CK), heads[h]], q_ref[pl.ds(q0, L), heads[h]], nt,
                                             preferred_element_type=jnp.float32)

    def advance(i, c):
        last = c == 0
        i2 = jnp.minimum(jnp.where(last, i + 1, i), nb - 1)
        c2 = jnp.where(last, jnp.minimum((i + 1) // CB, nb // CB - 1), c - 1)
        return i2, c2

    def item(i, c, slot, st):
        i2, c2 = advance(i, c)
        scores(i2, c2, 1 - slot)
        q0 = pl.multiple_of(i * L, L)
        k0 = pl.multiple_of(c * CK, CK)
        first = c == i // CB
        ct = ctile_scr[jnp.minimum(i - c * CB, CB)]
        out = []
        for h in range(hb):
            m, acc = st[h]
            m = jnp.where(first, NEG_INF, m)
            s = s_scr[slot, h] + ct
            sb = [s[t * L:(t + 1) * L, :] for t in range(CB)]
            b = [bias_scr[h, i, pl.ds(c * CB + t, 1), :] for t in range(CB)]
            m_new = m
            for t in range(CB):
                m_new = jnp.maximum(m_new, jnp.max(sb[t], axis=0, keepdims=True) + b[t])
            alpha = jnp.exp2(m - m_new)
            p = jnp.concatenate([jnp.exp2(sb[t] - (m_new - b[t])).astype(bf) for t in range(CB)], axis=0)
            acc = alpha * acc + jnp.dot(vta_scr[h, :, pl.ds(k0, CK)], p, preferred_element_type=jnp.float32)
            o_ref[pl.ds(q0, L), heads[h]] = (acc[:hd, :] / acc[hd:hd + 1, :]).T.astype(o_ref.dtype)
            out.append((m_new, acc))
        return (i2, c2, tuple(out))

    def two_items(_, carry):
        i, c, st = carry
        i, c, st = item(i, c, 0, st)
        return item(i, c, 1, st)

    scores(jnp.int32(0), jnp.int32(0), 0)
    init = tuple((jnp.zeros((1, L), jnp.float32), jnp.zeros((hd + ONES_ROWS, L), jnp.float32)) for _ in range(hb))
    lax.fori_loop(0, n_items // 2, two_items, (jnp.int32(0), jnp.int32(0), init))


def _attention(q, k, vt, kmean, hb=ATTN_HEADS_PER_STEP):
    B, S, W = q.shape
    H = W // HEAD_DIM
    nb = S // MOBA_BLOCK
    CB = ATTN_CHUNK_BLOCKS
    assert nb % CB == 0 and H % hb == 0 and sum(i // CB + 1 for i in range(nb)) % 2 == 0
    rows = pl.BlockSpec((None, S, hb * HEAD_DIM), lambda b, h: (b, 0, h))
    return pl.pallas_call(
        functools.partial(_attn_kernel, hb=hb),
        grid=(B, H // hb),
        in_specs=[rows, rows,
                  pl.BlockSpec((None, hb * HEAD_DIM, S), lambda b, h: (b, h, 0)),
                  pl.BlockSpec((None, nb, hb * HEAD_DIM), lambda b, h: (b, 0, h))],
        out_specs=rows,
        out_shape=jax.ShapeDtypeStruct((B, S, W), jnp.float32),
        scratch_shapes=[pltpu.VMEM((hb, HEAD_DIM + ONES_ROWS, S), jnp.bfloat16),
                        pltpu.VMEM((hb, nb, nb, MOBA_BLOCK), jnp.float32),
                        pltpu.VMEM((CB + 1, CB * MOBA_BLOCK, MOBA_BLOCK), jnp.float32),
                        pltpu.VMEM((2, hb, CB * MOBA_BLOCK, MOBA_BLOCK), jnp.float32)],
        compiler_params=_params(("arbitrary", "arbitrary")),
        name="moba_attention",
    )(q, k, vt, kmean)


def _outproj_kernel(a_ref, cv_ref, x_ref, mod_ref, ag_ref, w_ref, n2_ref, x1_ref, h2_ref):
    aw = a_ref.shape[-1]
    an = _rms(a_ref[...], ag_ref[...]).astype(jnp.bfloat16)
    y = (jnp.dot(an, w_ref[:aw, :], preferred_element_type=jnp.float32)
         + jnp.dot(cv_ref[...], w_ref[aw:, :], preferred_element_type=jnp.float32))
    x1 = x_ref[...] + mod_ref[2:3, :] * y
    x1_ref[...] = x1
    h2 = _rms(x1, n2_ref[...]) * (1.0 + mod_ref[4:5, :]) + mod_ref[3:4, :]
    h2_ref[...] = h2.astype(h2_ref.dtype)


def _outproj(attn, convn, x, mod, attn_out_g, w_out_bf, norm2_g, tm=512):
    B, S, D = x.shape
    aw = attn.shape[-1]
    row = lambda a: a.reshape(1, -1)
    rmap = lambda b, i: (b, i, 0)
    const = lambda b, i: (0, 0)
    return pl.pallas_call(
        _outproj_kernel,
        grid=(B, S // tm),
        in_specs=[pl.BlockSpec((None, tm, aw), rmap),
                  pl.BlockSpec((None, tm, convn.shape[-1]), rmap),
                  pl.BlockSpec((None, tm, D), rmap),
                  pl.BlockSpec((None, 6, D), lambda b, i: (b, 0, 0)),
                  pl.BlockSpec((1, aw), const),
                  pl.BlockSpec(w_out_bf.shape, const),
                  pl.BlockSpec((1, D), const)],
        out_specs=[pl.BlockSpec((None, tm, D), rmap),
                   pl.BlockSpec((None, tm, D), rmap)],
        out_shape=[jax.ShapeDtypeStruct((B, S, D), jnp.float32),
                   jax.ShapeDtypeStruct((B, S, D), jnp.bfloat16)],
        compiler_params=_params(("arbitrary", "arbitrary")),
        name="outproj",
    )(attn, convn, x, mod, row(attn_out_g), w_out_bf, row(norm2_g))


def _causal_conv3(u, tail_ref, cw_ref, cb_ref):
    tm = u.shape[0]
    row = lax.broadcasted_iota(jnp.int32, u.shape, 0)
    t1 = tail_ref[SUBLANES - 1:SUBLANES, :]
    t2 = tail_ref[SUBLANES - 2:SUBLANES - 1, :]
    u1 = jnp.where(row == 0, t1, pltpu.roll(u, 1, 0))
    u2 = jnp.where(row == 0, t2, jnp.where(row == 1, t1, pltpu.roll(u, 2, 0)))
    tail_ref[...] = u[tm - SUBLANES:, :]
    return cw_ref[0:1, :] * u2 + cw_ref[1:2, :] * u1 + cw_ref[2:3, :] * u + cb_ref[...]


FFN_CHAINS = 2


def _ffn_kernel(h2_ref, x1_ref, mod_ref, wg_ref, wu_ref, cwg_ref, cwu_ref, cbg_ref, cbu_ref, wd_ref,
                o_ref, acc_scr, tailg_scr, tailu_scr):
    i = pl.program_id(1)
    j = pl.program_id(2)
    tf = wg_ref.shape[1]
    n_chains = FFN_CHAINS
    cw = tf // n_chains

    @pl.when(i == 0)
    def _():
        tailg_scr[j] = jnp.zeros(tailg_scr.shape[1:], jnp.float32)
        tailu_scr[j] = jnp.zeros(tailu_scr.shape[1:], jnp.float32)

    @pl.when(j == 0)
    def _():
        acc_scr[...] = jnp.zeros_like(acc_scr)

    h2 = h2_ref[...]
    part = None
    for ch in range(n_chains):
        cs = slice(ch * cw, (ch + 1) * cw)
        ug = jnp.dot(h2, wg_ref[:, cs], preferred_element_type=jnp.float32)
        uu = jnp.dot(h2, wu_ref[:, cs], preferred_element_type=jnp.float32)
        cg = _causal_conv3(ug, tailg_scr.at[j, :, cs], cwg_ref.at[:, cs], cbg_ref.at[:, cs])
        cu = _causal_conv3(uu, tailu_scr.at[j, :, cs], cwu_ref.at[:, cs], cbu_ref.at[:, cs])
        act = (cg * (1.0 / (1.0 + jnp.exp(-cg))) * cu).astype(jnp.bfloat16)
        d = jnp.dot(act, wd_ref[cs, :], preferred_element_type=jnp.float32)
        part = d if part is None else part + d
    acc_scr[...] += part

    @pl.when(j == pl.num_programs(2) - 1)
    def _():
        o_ref[...] = x1_ref[...] + mod_ref[5:6, :] * acc_scr[...]


def _ffn(h2, x1, mod, w_up_bf, ffn_conv_w, ffn_conv_b, w_down_bf, tm=512, tf=512):
    B, S, D = x1.shape
    F = w_down_bf.shape[0]
    nf = F // tf
    cb = ffn_conv_b.reshape(1, -1)
    rmap = lambda b, i, j: (b, i, 0)
    gate_cols = lambda b, i, j: (0, j)
    up_cols = lambda b, i, j: (0, j + nf)
    return pl.pallas_call(
        _ffn_kernel,
        grid=(B, S // tm, nf),
        in_specs=[pl.BlockSpec((None, tm, D), rmap),
                  pl.BlockSpec((None, tm, D), rmap),
                  pl.BlockSpec((None, 6, D), lambda b, i, j: (b, 0, 0)),
                  pl.BlockSpec((D, tf), gate_cols),
                  pl.BlockSpec((D, tf), up_cols),
                  pl.BlockSpec((CONV_K, tf), gate_cols),
                  pl.BlockSpec((CONV_K, tf), up_cols),
                  pl.BlockSpec((1, tf), gate_cols),
                  pl.BlockSpec((1, tf), up_cols),
                  pl.BlockSpec((tf, D), lambda b, i, j: (j, 0))],
        out_specs=pl.BlockSpec((None, tm, D), rmap),
        out_shape=jax.ShapeDtypeStruct((B, S, D), jnp.float32),
        scratch_shapes=[pltpu.VMEM((tm, D), jnp.float32),
                        pltpu.VMEM((nf, SUBLANES, tf), jnp.float32),
                        pltpu.VMEM((nf, SUBLANES, tf), jnp.float32)],
        compiler_params=_params(("arbitrary", "arbitrary", "arbitrary")),
        name="convffn",
    )(h2, x1, mod, w_up_bf, w_up_bf, ffn_conv_w, ffn_conv_w, cb, cb, w_down_bf)


def kernel(x, c, w_ada, b_ada, norm1_g, w_in, q_norm_g, k_norm_g, conv_w, conv_b, attn_out_g, conv_out_g,
           w_out, norm2_g, w_ffn_up, ffn_conv_w, ffn_conv_b, w_ffn_down):
    B, S, D = x.shape
    depth = w_ada.shape[0]
    bf = jnp.bfloat16
    for l in range(depth):
        mod = _adaln(c, w_ada[l], b_ada[l]).reshape(B, 6, D)
        q, k, vt, kmean, convn = _inproj(x, mod, norm1_g[l], w_in[l].astype(bf), q_norm_g[l], k_norm_g[l],
                                         conv_w[l], conv_b[l], conv_out_g[l])
        attn = _attention(q, k, vt, kmean.reshape(B, S // MOBA_BLOCK, -1))
        x1, h2 = _outproj(attn, convn, x, mod, attn_out_g[l], w_out[l].astype(bf), norm2_g[l])
        x = _ffn(h2, x1, mod, w_ffn_up[l].astype(bf), ffn_conv_w[l], ffn_conv_b[l], w_ffn_down[l].astype(bf))
    return x
```

```python
import functools
import math

import jax
import jax.numpy as jnp
from jax import lax
from jax.experimental import pallas as pl
from jax.experimental.pallas import tpu as pltpu

HEAD_DIM = 128
MOBA_BLOCK = 256
MOBA_TOPK = 3
CONV_K = 3
EPS = 1e-6
NEG_INF = -1e30
LOG2E = 1.4426950408889634

V7X_VMEM_BYTES = 64 * 1024 * 1024
VMEM_LIMIT_BYTES = 56 * 1024 * 1024
SUBLANES = 8


def _params(semantics, flags=None):
    return pltpu.CompilerParams(dimension_semantics=semantics, vmem_limit_bytes=VMEM_LIMIT_BYTES, flags=flags)


def _rms(x, g):
    ms = jnp.mean(x * x, axis=-1, keepdims=True)
    return x * lax.rsqrt(ms + EPS) * g


def _adaln_kernel(c_ref, w_ref, b_ref, o_ref):
    c = c_ref[...]
    a = (c * (1.0 / (1.0 + jnp.exp(-c)))).astype(jnp.bfloat16)
    o_ref[...] = jnp.dot(a, w_ref[...].astype(jnp.bfloat16),
                         preferred_element_type=jnp.float32) + b_ref[...]


def _adaln(c, w_ada, b_ada, tn=1024):
    B, D = c.shape
    N = w_ada.shape[1]
    return pl.pallas_call(
        _adaln_kernel,
        grid=(N // tn,),
        in_specs=[pl.BlockSpec((B, D), lambda j: (0, 0)),
                  pl.BlockSpec((D, tn), lambda j: (0, j)),
                  pl.BlockSpec((1, tn), lambda j: (0, j))],
        out_specs=pl.BlockSpec((B, tn), lambda j: (0, j)),
        out_shape=jax.ShapeDtypeStruct((B, N), jnp.float32),
        compiler_params=_params(("arbitrary",)),
        name="adaln",
    )(c, w_ada, b_ada.reshape(1, N))


def _inproj_kernel(x_ref, mod_ref, n1_ref, w_ref, qg_ref, kg_ref, cw_ref, cb_ref, cog_ref,
                   q_ref, k_ref, vt_ref, km_ref, cv_ref,
                   h_scr, tail_scr, *, tm, scale):
    i = pl.program_id(1)
    gw = q_ref.shape[-1]
    n_heads = gw // HEAD_DIM

    @pl.when(i == 0)
    def _():
        tail_scr[...] = jnp.zeros_like(tail_scr)

    y = _rms(x_ref[...], n1_ref[...])
    h_scr[...] = (y * (1.0 + mod_ref[1:2, :]) + mod_ref[0:1, :]).astype(jnp.bfloat16)

    def proj(g):
        return jnp.dot(h_scr[...], w_ref[g], preferred_element_type=jnp.float32)

    rq = proj(0)
    rk = proj(1)
    for hh in range(n_heads):
        sl = slice(hh * HEAD_DIM, (hh + 1) * HEAD_DIM)
        q_ref[:, sl] = (_rms(rq[:, sl], qg_ref[...]) * scale).astype(q_ref.dtype)
    rv = proj(2)
    for hh in range(n_heads):
        sl = slice(hh * HEAD_DIM, (hh + 1) * HEAD_DIM)
        kn = _rms(rk[:, sl], kg_ref[...])
        k_ref[:, sl] = kn.astype(k_ref.dtype)
        for blk in range(tm // MOBA_BLOCK):
            rows = kn[blk * MOBA_BLOCK:(blk + 1) * MOBA_BLOCK, :]
            km_ref[blk, :, sl] = jnp.mean(rows, axis=0, keepdims=True)
    rgb = proj(3)
    vt_ref[...] = rv.T.astype(vt_ref.dtype)
    rgc = proj(4)
    rxt = proj(5)
    yc = _causal_conv3(rgc * rxt, tail_scr, cw_ref, cb_ref)
    cv_ref[...] = _rms(rgb * yc, cog_ref[...]).astype(cv_ref.dtype)


def _inproj(x, mod, norm1_g, w_in_bf, q_norm_g, k_norm_g, conv_w, conv_b, conv_out_g, tm=512):
    B, S, D = x.shape
    n_groups, _, gw = w_in_bf.shape
    nb = S // MOBA_BLOCK
    row = lambda a: a.reshape(1, -1)
    xmap = lambda b, i: (b, i, 0)
    const = lambda b, i: (0, 0)
    bf = jnp.bfloat16
    return pl.pallas_call(
        functools.partial(_inproj_kernel, tm=tm, scale=LOG2E / math.sqrt(HEAD_DIM)),
        grid=(B, S // tm),
        in_specs=[pl.BlockSpec((None, tm, D), xmap),
                  pl.BlockSpec((None, 6, D), lambda b, i: (b, 0, 0)),
                  pl.BlockSpec((1, D), const),
                  pl.BlockSpec((n_groups, D, gw), lambda b, i: (0, 0, 0),
                               pipeline_mode=pl.Buffered(1)),
                  pl.BlockSpec((1, HEAD_DIM), const),
                  pl.BlockSpec((1, HEAD_DIM), const),
                  pl.BlockSpec((CONV_K, gw), const),
                  pl.BlockSpec((1, gw), const),
                  pl.BlockSpec((1, gw), const)],
        out_specs=[pl.BlockSpec((None, tm, gw), xmap),
                   pl.BlockSpec((None, tm, gw), xmap),
                   pl.BlockSpec((None, gw, tm), lambda b, i: (b, 0, i)),
                   pl.BlockSpec((None, tm // MOBA_BLOCK, 1, gw), lambda b, i: (b, i, 0, 0)),
                   pl.BlockSpec((None, tm, gw), xmap)],
        out_shape=[jax.ShapeDtypeStruct((B, S, gw), bf),
                   jax.ShapeDtypeStruct((B, S, gw), bf),
                   jax.ShapeDtypeStruct((B, gw, S), bf),
                   jax.ShapeDtypeStruct((B, nb, 1, gw), jnp.float32),
                   jax.ShapeDtypeStruct((B, S, gw), bf)],
        scratch_shapes=[pltpu.VMEM((tm, D), bf),
                        pltpu.VMEM((SUBLANES, gw), jnp.float32)],
        compiler_params=_params(("arbitrary", "arbitrary")),
        name="inproj",
    )(x, mod, row(norm1_g), w_in_bf, row(q_norm_g), row(k_norm_g), conv_w, row(conv_b), row(conv_out_g))


ATTN_CHUNK_BLOCKS = 4
ATTN_HEADS_PER_STEP = 2
ONES_ROWS = 16


def _attn_kernel(q_ref, k_ref, vt_ref, km_ref, o_ref, vta_scr, bias_scr, ctile_scr, s_scr, *, hb):
    L = MOBA_BLOCK
    CB = ATTN_CHUNK_BLOCKS
    CK = CB * L
    hd = HEAD_DIM
    S = q_ref.shape[0]
    nb = S // L
    n_items = sum(i // CB + 1 for i in range(nb))
    nt = (((1,), (1,)), ((), ()))
    bf = jnp.bfloat16
    heads = [slice(h * hd, (h + 1) * hd) for h in range(hb)]

    for h in range(hb):
        vta_scr[h, :hd, :] = vt_ref[heads[h], :]
        vta_scr[h, hd:, :] = jnp.ones((ONES_ROWS, S), bf)

    rel = lax.broadcasted_iota(jnp.int32, (CK, L), 0) - lax.broadcasted_iota(jnp.int32, (CK, L), 1)
    for d in range(CB + 1):
        ctile_scr[d] = jnp.where(rel <= d * L, 0.0, NEG_INF) if d < CB else jnp.zeros((CK, L), jnp.float32)

    blk_id = lax.broadcasted_iota(jnp.int32, (nb, L), 0)
    for h in range(hb):
        gate_all = lax.dot_general(km_ref[:, heads[h]], q_ref[:, heads[h]].astype(jnp.float32), nt,
                                   preferred_element_type=jnp.float32)
        for i in range(nb):
            past = blk_id < i
            gate = jnp.where(past, gate_all[:, i * L:(i + 1) * L], NEG_INF)
            rank = jnp.zeros((nb, L), jnp.int32)
            for n in range(i):
                gn = gate[n:n + 1, :]
                beats = (gn > gate) | ((gn == gate) & (n < blk_id))
                rank = rank + beats.astype(jnp.int32)
            bias = jnp.where((past & (rank < MOBA_TOPK)) | (blk_id == i), 0.0, NEG_INF)
            bias_scr[h, i] = bias

    def scores(i, c, slot):
        q0 = pl.multiple_of(i * L, L)
        k0 = pl.multiple_of(c * CK, CK)
        for h in range(hb):
            s_scr[slot, h] = lax.dot_general(k_ref[pl.ds(k0, CK), heads[h]], q_ref[pl.ds(q0, L), heads[h]], nt,
                                             preferred_element_type=jnp.float32)

    def advance(i, c):
        last = c == 0
        i2 = jnp.minimum(jnp.where(last, i + 1, i), nb - 1)
        c2 = jnp.where(last, jnp.minimum((i + 1) // CB, nb // CB - 1), c - 1)
        return i2, c2

    def item(i, c, slot, st):
        i2, c2 = advance(i, c)
        scores(i2, c2, 1 - slot)
        q0 = pl.multiple_of(i * L, L)
        k0 = pl.multiple_of(c * CK, CK)
        first = c == i // CB
        ct = ctile_scr[jnp.minimum(i - c * CB, CB)]
        out = []
        for h in range(hb):
            m, acc = st[h]
            m = jnp.where(first, NEG_INF, m)
            s = s_scr[slot, h] + ct
            sb = [s[t * L:(t + 1) * L, :] for t in range(CB)]
            b = [bias_scr[h, i, pl.ds(c * CB + t, 1), :] for t in range(CB)]
            m_new = m
            for t in range(CB):
                m_new = jnp.maximum(m_new, jnp.max(sb[t], axis=0, keepdims=True) + b[t])
            alpha = jnp.exp2(m - m_new)
            p = jnp.concatenate([jnp.exp2(sb[t] - (m_new - b[t])).astype(bf) for t in range(CB)], axis=0)
            acc = alpha * acc + jnp.dot(vta_scr[h, :, pl.ds(k0, CK)], p, preferred_element_type=jnp.float32)
            o_ref[pl.ds(q0, L), heads[h]] = (acc[:hd, :] / acc[hd:hd + 1, :]).T.astype(o_ref.dtype)
            out.append((m_new, acc))
        return (i2, c2, tuple(out))

    def two_items(_, carry):
        i, c, st = carry
        i, c, st = item(i, c, 0, st)
        return item(i, c, 1, st)

    scores(jnp.int32(0), jnp.int32(0), 0)
    init = tuple((jnp.zeros((1, L), jnp.float32), jnp.zeros((hd + ONES_ROWS, L), jnp.float32)) for _ in range(hb))
    lax.fori_loop(0, n_items // 2, two_items, (jnp.int32(0), jnp.int32(0), init))


def _attention(q, k, vt, kmean, hb=ATTN_HEADS_PER_STEP):
    B, S, W = q.shape
    H = W // HEAD_DIM
    nb = S // MOBA_BLOCK
    CB = ATTN_CHUNK_BLOCKS
    assert nb % CB == 0 and H % hb == 0 and sum(i // CB + 1 for i in range(nb)) % 2 == 0
    rows = pl.BlockSpec((None, S, hb * HEAD_DIM), lambda b, h: (b, 0, h))
    return pl.pallas_call(
        functools.partial(_attn_kernel, hb=hb),
        grid=(B, H // hb),
        in_specs=[rows, rows,
                  pl.BlockSpec((None, hb * HEAD_DIM, S), lambda b, h: (b, h, 0)),
                  pl.BlockSpec((None, nb, hb * HEAD_DIM), lambda b, h: (b, 0, h))],
        out_specs=rows,
        out_shape=jax.ShapeDtypeStruct((B, S, W), jnp.float32),
        scratch_shapes=[pltpu.VMEM((hb, HEAD_DIM + ONES_ROWS, S), jnp.bfloat16),
                        pltpu.VMEM((hb, nb, nb, MOBA_BLOCK), jnp.float32),
                        pltpu.VMEM((CB + 1, CB * MOBA_BLOCK, MOBA_BLOCK), jnp.float32),
                        pltpu.VMEM((2, hb, CB * MOBA_BLOCK, MOBA_BLOCK), jnp.float32)],
        compiler_params=_params(("arbitrary", "arbitrary")),
        name="moba_attention",
    )(q, k, vt, kmean)


def _outproj_kernel(a_ref, cv_ref, x_ref, mod_ref, ag_ref, w_ref, n2_ref, x1_ref, h2_ref):
    aw = a_ref.shape[-1]
    an = _rms(a_ref[...], ag_ref[...]).astype(jnp.bfloat16)
    y = (jnp.dot(an, w_ref[:aw, :], preferred_element_type=jnp.float32)
         + jnp.dot(cv_ref[...], w_ref[aw:, :], preferred_element_type=jnp.float32))
    x1 = x_ref[...] + mod_ref[2:3, :] * y
    x1_ref[...] = x1
    h2 = _rms(x1, n2_ref[...]) * (1.0 + mod_ref[4:5, :]) + mod_ref[3:4, :]
    h2_ref[...] = h2.astype(h2_ref.dtype)


def _outproj(attn, convn, x, mod, attn_out_g, w_out_bf, norm2_g, tm=512):
    B, S, D = x.shape
    aw = attn.shape[-1]
    row = lambda a: a.reshape(1, -1)
    rmap = lambda b, i: (b, i, 0)
    const = lambda b, i: (0, 0)
    return pl.pallas_call(
        _outproj_kernel,
        grid=(B, S // tm),
        in_specs=[pl.BlockSpec((None, tm, aw), rmap),
                  pl.BlockSpec((None, tm, convn.shape[-1]), rmap),
                  pl.BlockSpec((None, tm, D), rmap),
                  pl.BlockSpec((None, 6, D), lambda b, i: (b, 0, 0)),
                  pl.BlockSpec((1, aw), const),
                  pl.BlockSpec(w_out_bf.shape, const),
                  pl.BlockSpec((1, D), const)],
        out_specs=[pl.BlockSpec((None, tm, D), rmap),
                   pl.BlockSpec((None, tm, D), rmap)],
        out_shape=[jax.ShapeDtypeStruct((B, S, D), jnp.float32),
                   jax.ShapeDtypeStruct((B, S, D), jnp.bfloat16)],
        compiler_params=_params(("arbitrary", "arbitrary")),
        name="outproj",
    )(attn, convn, x, mod, row(attn_out_g), w_out_bf, row(norm2_g))


def _causal_conv3(u, tail_ref, cw_ref, cb_ref):
    tm = u.shape[0]
    y = _conv3_rows(u, tail_ref[...], cw_ref, cb_ref)
    tail_ref[...] = u[tm - SUBLANES:, :]
    return y


def _conv3_rows(u, prev, cw_ref, cb_ref):
    m, n = u.shape
    g = m // SUBLANES
    full = jnp.concatenate([prev.reshape(1, SUBLANES, n), u.reshape(g, SUBLANES, n)], axis=0)
    r1 = pltpu.roll(full, 1, 1)
    r2 = pltpu.roll(full, 2, 1)
    sub = lax.broadcasted_iota(jnp.int32, (g, SUBLANES, n), 1)
    u1 = jnp.where(sub >= 1, r1[1:], r1[:-1])
    u2 = jnp.where(sub >= 2, r2[1:], r2[:-1])
    w = cw_ref[...]
    y = (w[0:1, :].reshape(1, 1, n) * u2 + w[1:2, :].reshape(1, 1, n) * u1
         + w[2:3, :].reshape(1, 1, n) * full[1:] + cb_ref[...].reshape(1, 1, n))
    return y.reshape(m, n)


FFN_CHAINS = 2
FFN_ROW_CHAINS = 1


def _ffn_kernel(h2_ref, x1_ref, mod_ref, wg_ref, wu_ref, cwg_ref, cwu_ref, cbg_ref, cbu_ref, wd_ref,
                o_ref, acc_scr, tailg_scr, tailu_scr):
    i = pl.program_id(1)
    j = pl.program_id(2)
    tf = wg_ref.shape[1]
    n_chains = FFN_CHAINS
    cw = tf // n_chains

    @pl.when(i == 0)
    def _():
        tailg_scr[j] = jnp.zeros(tailg_scr.shape[1:], jnp.float32)
        tailu_scr[j] = jnp.zeros(tailu_scr.shape[1:], jnp.float32)

    @pl.when(j == 0)
    def _():
        acc_scr[...] = jnp.zeros_like(acc_scr)

    tm = h2_ref.shape[0]
    rm = tm // FFN_ROW_CHAINS
    cols = [slice(ch * cw, (ch + 1) * cw) for ch in range(n_chains)]
    prev = [(tailg_scr[j, :, cs], tailu_scr[j, :, cs]) for cs in cols]
    for rh in range(FFN_ROW_CHAINS):
        rows = slice(rh * rm, (rh + 1) * rm)
        h2 = h2_ref[rows, :]
        part = None
        for ch, cs in enumerate(cols):
            ug = jnp.dot(h2, wg_ref[:, cs], preferred_element_type=jnp.float32)
            uu = jnp.dot(h2, wu_ref[:, cs], preferred_element_type=jnp.float32)
            cg = _conv3_rows(ug, prev[ch][0], cwg_ref.at[:, cs], cbg_ref.at[:, cs])
            cu = _conv3_rows(uu, prev[ch][1], cwu_ref.at[:, cs], cbu_ref.at[:, cs])
            prev[ch] = (ug[rm - SUBLANES:, :], uu[rm - SUBLANES:, :])
            act = (cg * (1.0 / (1.0 + jnp.exp(-cg))) * cu).astype(jnp.bfloat16)
            d = jnp.dot(act, wd_ref[cs, :], preferred_element_type=jnp.float32)
            part = d if part is None else part + d
        acc_scr[rows, :] += part
    for ch, cs in enumerate(cols):
        tailg_scr[j, :, cs] = prev[ch][0]
        tailu_scr[j, :, cs] = prev[ch][1]

    @pl.when(j == pl.num_programs(2) - 1)
    def _():
        o_ref[...] = x1_ref[...] + mod_ref[5:6, :] * acc_scr[...]


def _ffn(h2, x1, mod, w_up_bf, ffn_conv_w, ffn_conv_b, w_down_bf, tm=512, tf=512):
    B, S, D = x1.shape
    F = w_down_bf.shape[0]
    nf = F // tf
    cb = ffn_conv_b.reshape(1, -1)
    rmap = lambda b, i, j: (b, i, 0)
    gate_cols = lambda b, i, j: (0, j)
    up_cols = lambda b, i, j: (0, j + nf)
    return pl.pallas_call(
        _ffn_kernel,
        grid=(B, S // tm, nf),
        in_specs=[pl.BlockSpec((None, tm, D), rmap),
                  pl.BlockSpec((None, tm, D), rmap),
                  pl.BlockSpec((None, 6, D), lambda b, i, j: (b, 0, 0)),
                  pl.BlockSpec((D, tf), gate_cols),
                  pl.BlockSpec((D, tf), up_cols),
                  pl.BlockSpec((CONV_K, tf), gate_cols),
                  pl.BlockSpec((CONV_K, tf), up_cols),
                  pl.BlockSpec((1, tf), gate_cols),
                  pl.BlockSpec((1, tf), up_cols),
                  pl.BlockSpec((tf, D), lambda b, i, j: (j, 0))],
        out_specs=pl.BlockSpec((None, tm, D), rmap),
        out_shape=jax.ShapeDtypeStruct((B, S, D), jnp.float32),
        scratch_shapes=[pltpu.VMEM((tm, D), jnp.float32),
                        pltpu.VMEM((nf, SUBLANES, tf), jnp.float32),
                        pltpu.VMEM((nf, SUBLANES, tf), jnp.float32)],
        compiler_params=_params(("arbitrary", "arbitrary", "arbitrary")),
        name="convffn",
    )(h2, x1, mod, w_up_bf, w_up_bf, ffn_conv_w, ffn_conv_w, cb, cb, w_down_bf)


def kernel(x, c, w_ada, b_ada, norm1_g, w_in, q_norm_g, k_norm_g, conv_w, conv_b, attn_out_g, conv_out_g,
           w_out, norm2_g, w_ffn_up, ffn_conv_w, ffn_conv_b, w_ffn_down):
    B, S, D = x.shape
    depth = w_ada.shape[0]
    bf = jnp.bfloat16
    for l in range(depth):
        mod = _adaln(c, w_ada[l], b_ada[l]).reshape(B, 6, D)
        w_in3 = w_in[l].reshape(D, 6, -1).transpose(1, 0, 2).astype(bf)
        q, k, vt, kmean, convn = _inproj(x, mod, norm1_g[l], w_in3, q_norm_g[l], k_norm_g[l],
                                         conv_w[l], conv_b[l], conv_out_g[l])
        attn = _attention(q, k, vt, kmean.reshape(B, S // MOBA_BLOCK, -1))
        x1, h2 = _outproj(attn, convn, x, mod, attn_out_g[l], w_out[l].astype(bf), norm2_g[l])
        x = _ffn(h2, x1, mod, w_ffn_up[l].astype(bf), ffn_conv_w[l], ffn_conv_b[l], w_ffn_down[l].astype(bf))
    return x
```
